```python
import math
import jax, jax.numpy as jnp
from jax import lax
import numpy as np

D_MODEL = 2048
BATCH = 2
SEQ = 16384
DEPTH = 2

HEAD_DIM = 64
N_Q_HEADS = D_MODEL // HEAD_DIM
N_KV_HEADS = N_Q_HEADS // 8
Q_PER_KV = N_Q_HEADS // N_KV_HEADS
WINDOW = 128
BLOCK = 128
QKV_WIDTH = (N_Q_HEADS + 2 * N_KV_HEADS) * HEAD_DIM
NUM_BUCKETS = 32
MAX_DISTANCE = 128
GMLP_CHUNK = 128
GMLP_INNER = D_MODEL
GMLP_GROUPS = 16
GMLP_GROUP_DIM = GMLP_INNER // GMLP_GROUPS
FFN_HIDDEN = -(-8 * D_MODEL // (3 * 256)) * 256
N_MIXERS = 2
N_ATTN_LAYERS = (DEPTH + 1) // 2
N_GMLP_LAYERS = DEPTH // 2
NORM_EPS = 1e-6
LN_EPS = 1e-5
NEG_INF = -1e30

kernel_name = "hybrid_swa_sink_gmlp_sandwich"


def rmsnorm(x, gain):
    xf = x.astype(jnp.float32)
    y = xf * lax.rsqrt(jnp.mean(xf * xf, axis=-1, keepdims=True) + NORM_EPS)
    return (y * gain.astype(jnp.float32)).astype(x.dtype)


def layernorm(x, gain, bias):
    xf = x.astype(jnp.float32)
    mu = jnp.mean(xf, axis=-1, keepdims=True)
    xc = xf - mu
    y = xc * lax.rsqrt(jnp.mean(xc * xc, axis=-1, keepdims=True) + LN_EPS)
    return (y * gain.astype(jnp.float32) + bias.astype(jnp.float32)).astype(x.dtype)


def t5_causal_bucket(dist):
    max_exact = NUM_BUCKETS // 2
    is_small = dist < max_exact
    d_f = jnp.maximum(dist, 1).astype(jnp.float32)
    large = max_exact + (jnp.log(d_f / max_exact) / math.log(MAX_DISTANCE / max_exact)
                         * (NUM_BUCKETS - max_exact)).astype(jnp.int32)
    large = jnp.minimum(large, NUM_BUCKETS - 1)
    return jnp.where(is_small, dist, large)


def swa_sink_attention(h, w_qkv, w_o, sinks, rel_bias_table):
    B, S, _ = h.shape
    nb = S // BLOCK
    qkv = h @ w_qkv
    q_w = N_Q_HEADS * HEAD_DIM
    kv_w = N_KV_HEADS * HEAD_DIM
    q = qkv[..., :q_w].reshape(B, nb, BLOCK, N_KV_HEADS, Q_PER_KV, HEAD_DIM)
    k = qkv[..., q_w:q_w + kv_w].reshape(B, S, N_KV_HEADS, HEAD_DIM)
    v = qkv[..., q_w + kv_w:].reshape(B, S, N_KV_HEADS, HEAD_DIM)

    def band(t):
        prev = jnp.pad(t, ((0, 0), (BLOCK, 0), (0, 0), (0, 0)))[:, :S]
        return jnp.concatenate([prev.reshape(B, nb, BLOCK, N_KV_HEADS, HEAD_DIM),
                                t.reshape(B, nb, BLOCK, N_KV_HEADS, HEAD_DIM)], axis=2)

    kb, vb = band(k), band(v)
    scale = HEAD_DIM ** -0.5
    s = jnp.einsum('bnqkgd,bnjkd->bnkgqj', q, kb).astype(jnp.float32) * scale

    qi = jnp.arange(BLOCK, dtype=jnp.int32)[:, None] + BLOCK
    kj = jnp.arange(2 * BLOCK, dtype=jnp.int32)[None, :]
    dist = qi - kj
    bucket = t5_causal_bucket(jnp.maximum(dist, 0))
    bias = jnp.transpose(rel_bias_table[bucket].astype(jnp.float32), (2, 0, 1))
    bias = bias.reshape(N_KV_HEADS, Q_PER_KV, BLOCK, 2 * BLOCK)
    in_window = (dist >= 0) & (dist < WINDOW)
    blk = jnp.arange(nb, dtype=jnp.int32)[:, None, None]
    key_exists = (blk * BLOCK + kj[None] - BLOCK) >= 0
    mask = in_window[None] & key_exists
    s = jnp.where(mask[None, :, None, None], s + bias[None, None], NEG_INF)

    sink = sinks.astype(jnp.float32).reshape(1, 1, N_KV_HEADS, Q_PER_KV, 1, 1)
    m = jnp.maximum(jnp.max(s, axis=-1, keepdims=True), sink)
    p = jnp.exp(s - m)
    probs = p / (jnp.sum(p, axis=-1, keepdims=True) + jnp.exp(sink - m))
    o = jnp.einsum('bnkgqj,bnjkd->bnqkgd', probs.astype(vb.dtype), vb)
    return o.reshape(B, S, q_w) @ w_o


def chunked_gmlp(h, w_in, ln_gain, ln_bias, w_spatial, b_spatial, w_out):
    B, S, _ = h.shape
    nc = S // GMLP_CHUNK
    z = jax.nn.gelu(h @ w_in)
    u, v = z[..., :GMLP_INNER], z[..., GMLP_INNER:]
    v = layernorm(v, ln_gain, ln_bias)
    v = v.reshape(B, nc, GMLP_CHUNK, GMLP_GROUPS, GMLP_GROUP_DIM)
    causal = jnp.tril(jnp.ones((GMLP_CHUNK, GMLP_CHUNK), dtype=bool))
    w_s = jnp.where(causal[None], w_spatial, jnp.zeros_like(w_spatial))
    mixed = jnp.einsum('gts,bnsgc->bntgc', w_s, v) + b_spatial.T[:, :, None]
    gated = u * mixed.reshape(B, S, GMLP_INNER)
    return gated @ w_out


def swiglu(h, w_gate_up, w_down):
    gu = h @ w_gate_up
    g, up = gu[..., :FFN_HIDDEN], gu[..., FFN_HIDDEN:]
    return (jax.nn.silu(g) * up) @ w_down


def setup_inputs(seed: int = 0) -> dict:
    key = jax.random.key(seed)
    ks = jax.random.split(key, 16)
    f32 = jnp.float32

    def dense(k, shape, fan_in):
        return jax.random.normal(k, shape, f32) * fan_in ** -0.5

    x = jax.random.normal(ks[0], (BATCH, SEQ, D_MODEL), f32)
    attn_w_qkv = dense(ks[1], (N_ATTN_LAYERS, D_MODEL, QKV_WIDTH), D_MODEL)
    attn_w_o = dense(ks[2], (N_ATTN_LAYERS, N_Q_HEADS * HEAD_DIM, D_MODEL), N_Q_HEADS * HEAD_DIM)
    attn_sinks = jax.random.normal(ks[3], (N_ATTN_LAYERS, N_Q_HEADS), f32)
    rel_bias_table = 0.5 * jax.random.normal(ks[4], (NUM_BUCKETS, N_Q_HEADS), f32)
    gmlp_w_in = dense(ks[5], (N_GMLP_LAYERS, D_MODEL, 2 * GMLP_INNER), D_MODEL)
    gmlp_ln_gain = 1.0 + 0.1 * jax.random.normal(ks[6], (N_GMLP_LAYERS, GMLP_INNER), f32)
    gmlp_ln_bias = 0.1 * jax.random.normal(ks[7], (N_GMLP_LAYERS, GMLP_INNER), f32)
    gmlp_w_spatial = dense(ks[8], (N_GMLP_LAYERS, GMLP_GROUPS, GMLP_CHUNK, GMLP_CHUNK), GMLP_CHUNK)
    gmlp_b_spatial = 1.0 + 0.1 * jax.random.normal(ks[9], (N_GMLP_LAYERS, GMLP_GROUPS, GMLP_CHUNK), f32)
    gmlp_w_out = dense(ks[10], (N_GMLP_LAYERS, GMLP_INNER, D_MODEL), GMLP_INNER)
    norm_gains = 1.0 + 0.1 * jax.random.normal(ks[11], (DEPTH, 4, D_MODEL), f32)
    ffn_w_gate_up = dense(ks[12], (DEPTH, D_MODEL, 2 * FFN_HIDDEN), D_MODEL)
    ffn_w_down = dense(ks[13], (DEPTH, FFN_HIDDEN, D_MODEL), FFN_HIDDEN)
    return {"x": x, "attn_w_qkv": attn_w_qkv, "attn_w_o": attn_w_o, "attn_sinks": attn_sinks,
            "rel_bias_table": rel_bias_table, "gmlp_w_in": gmlp_w_in, "gmlp_ln_gain": gmlp_ln_gain,
            "gmlp_ln_bias": gmlp_ln_bias, "gmlp_w_spatial": gmlp_w_spatial,
            "gmlp_b_spatial": gmlp_b_spatial, "gmlp_w_out": gmlp_w_out, "norm_gains": norm_gains,
            "ffn_w_gate_up": ffn_w_gate_up, "ffn_w_down": ffn_w_down}


def reference(x, attn_w_qkv, attn_w_o, attn_sinks, rel_bias_table, gmlp_w_in, gmlp_ln_gain,
              gmlp_ln_bias, gmlp_w_spatial, gmlp_b_spatial, gmlp_w_out, norm_gains,
              ffn_w_gate_up, ffn_w_down):
    for i in range(DEPTH):
        h = rmsnorm(x, norm_gains[i, 0])
        j = i // N_MIXERS
        if i % N_MIXERS == 0:
            mix = swa_sink_attention(h, attn_w_qkv[j], attn_w_o[j], attn_sinks[j], rel_bias_table)
        else:
            mix = chunked_gmlp(h, gmlp_w_in[j], gmlp_ln_gain[j], gmlp_ln_bias[j],
                               gmlp_w_spatial[j], gmlp_b_spatial[j], gmlp_w_out[j])
        x = x + rmsnorm(mix, norm_gains[i, 1])
        h = rmsnorm(x, norm_gains[i, 2])
        x = x + rmsnorm(swiglu(h, ffn_w_gate_up[i], ffn_w_down[i]), norm_gains[i, 3])
    return x
```

```python
import functools
import math

import jax
import jax.numpy as jnp
from jax import lax
from jax.experimental import pallas as pl
from jax.experimental.pallas import tpu as pltpu

F32 = jnp.float32
BF16 = jnp.bfloat16

D_MODEL = 2048
HEAD_DIM = 64
N_Q_HEADS = 32
N_KV_HEADS = 4
Q_PER_KV = 8
BLOCK = 128
Q_WIDTH = N_Q_HEADS * HEAD_DIM
KV_WIDTH = N_KV_HEADS * HEAD_DIM
QKV_WIDTH = Q_WIDTH + 2 * KV_WIDTH
NUM_BUCKETS = 32
MAX_DISTANCE = 128
GMLP_GROUPS = 16
GROUP_DIM = 128
FFN_HIDDEN = 5632
NORM_EPS = 1e-6
LN_EPS = 1e-5
NEG_INF = -1e30
N_KV_PAIRS = N_KV_HEADS // 2
PAIR_ROWS = Q_PER_KV * BLOCK

VMEM_LIMIT = 56 * 1024 * 1024


def _params(n_axes, vmem=VMEM_LIMIT):
    return pltpu.CompilerParams(dimension_semantics=("arbitrary",) * n_axes,
                                vmem_limit_bytes=vmem)


def _resident(shape):
    nd = len(shape)
    return pl.BlockSpec(shape, lambda *_: (0,) * nd, pipeline_mode=pl.Buffered(1))


def _rms(x, gain):
    ms = jnp.mean(x * x, axis=-1, keepdims=True)
    return x * lax.rsqrt(ms + NORM_EPS) * gain


def _rel_bias_kernel(table_ref, bucket_ref, out_ref):
    h = pl.program_id(1)
    bucket = bucket_ref[0]
    acc = jnp.full(bucket.shape, NEG_INF, F32)
    for b in range(NUM_BUCKETS):
        acc = jnp.where(bucket == b, table_ref[b, h], acc)
    out_ref[0, 0] = acc


def _rel_bias(rel_bias_table):
    qi = jnp.arange(BLOCK, dtype=jnp.int32)[:, None] + BLOCK
    kj = jnp.arange(2 * BLOCK, dtype=jnp.int32)[None, :]
    dist = qi - kj
    d = jnp.maximum(dist, 0)
    max_exact = NUM_BUCKETS // 2
    d_f = jnp.maximum(d, 1).astype(F32)
    large = max_exact + (jnp.log(d_f / max_exact) / math.log(MAX_DISTANCE / max_exact)
                         * (NUM_BUCKETS - max_exact)).astype(jnp.int32)
    large = jnp.minimum(large, NUM_BUCKETS - 1)
    bucket = jnp.where(d < max_exact, d, large)
    in_window = (dist >= 0) & (dist < BLOCK)
    normal = jnp.where(in_window, bucket, -1)
    first = jnp.where(kj >= BLOCK, normal, -1)
    buckets = jnp.stack([normal, first]).astype(jnp.int32)

    def out_index(v, h):
        kvh = h // Q_PER_KV
        return (v, kvh // 2, h % Q_PER_KV, kvh % 2)

    out = pl.pallas_call(
        _rel_bias_kernel,
        out_shape=jax.ShapeDtypeStruct((2, N_KV_PAIRS, PAIR_ROWS, 4 * BLOCK), F32),
        grid=(2, N_Q_HEADS),
        in_specs=[pl.BlockSpec(memory_space=pltpu.SMEM),
                  pl.BlockSpec((1, BLOCK, 2 * BLOCK), lambda v, h: (v, 0, 0))],
        out_specs=pl.BlockSpec((1, 1, BLOCK, 2 * BLOCK), out_index),
        compiler_params=_params(2),
        name="rel_bias",
    )(rel_bias_table, buckets)
    return out


def _qkv_kernel(x_ref, g_ref, w_ref, o_ref, h_scr, *, n_chunk):
    h_scr[...] = _rms(x_ref[...], g_ref[...]).astype(BF16)
    for n0 in range(0, o_ref.shape[1], n_chunk):
        o_ref[:, n0:n0 + n_chunk] = jnp.dot(
            h_scr[...], w_ref[:, n0:n0 + n_chunk], preferred_element_type=F32).astype(BF16)


def _qkv_proj(x2, gain, w, tm=1024, n_chunk=512):
    t, d = x2.shape
    n = w.shape[1]
    return pl.pallas_call(
        functools.partial(_qkv_kernel, n_chunk=n_chunk),
        out_shape=jax.ShapeDtypeStruct((t, n), BF16),
        grid=(t // tm,),
        in_specs=[pl.BlockSpec((tm, d), lambda i: (i, 0)),
                  _resident((1, d)),
                  _resident((d, n))],
        out_specs=pl.BlockSpec((tm, n), lambda i: (i, 0)),
        scratch_shapes=[pltpu.VMEM((tm, d), BF16)],
        compiler_params=_params(1),
        name="qkv_proj",
    )(x2, gain, w)


def _attn_kernel(sink_ref, q_ref, k_ref, v_ref, kp_ref, vp_ref, bias_ref, o_ref, kbuf, vbuf, p_scr,
                 *, n_blocks):
    i = pl.program_id(1)
    tq = q_ref.shape[0]
    has_prev = i > 0
    kbuf[0:BLOCK] = jnp.where(has_prev, kp_ref[...], jnp.zeros_like(kp_ref))
    vbuf[0:BLOCK] = jnp.where(has_prev, vp_ref[...], jnp.zeros_like(vp_ref))
    kbuf[BLOCK:BLOCK + tq] = k_ref[...]
    vbuf[BLOCK:BLOCK + tq] = v_ref[...]

    lane = lax.broadcasted_iota(jnp.int32, (2 * BLOCK, 2 * HEAD_DIM), 1)
    lo = lane < HEAD_DIM
    row = lax.broadcasted_iota(jnp.int32, (4 * BLOCK, 2 * HEAD_DIM), 0)
    lane2 = lax.broadcasted_iota(jnp.int32, (4 * BLOCK, 2 * HEAD_DIM), 1)
    ones_bd = jnp.where((row < 2 * BLOCK) == (lane2 < HEAD_DIM), 1.0, 0.0).astype(BF16)
    lane_o = lax.broadcasted_iota(jnp.int32, (BLOCK, 2 * HEAD_DIM), 1)
    lo_o = lane_o < HEAD_DIM

    def block(j, carry):
        r0 = pl.multiple_of(j * BLOCK, BLOCK)
        variant = jnp.where(jnp.logical_and(i == 0, j == 0), 1, 0)
        for pair in range(N_KV_PAIRS):
            c0 = pair * 2 * HEAD_DIM
            kk = kbuf[pl.ds(r0, 2 * BLOCK), c0:c0 + 2 * HEAD_DIM]
            vv = vbuf[pl.ds(r0, 2 * BLOCK), c0:c0 + 2 * HEAD_DIM]
            zero = jnp.zeros_like(kk)
            k_bd = jnp.concatenate([jnp.where(lo, kk, zero), jnp.where(lo, zero, kk)], axis=0)
            v_bd = jnp.concatenate([jnp.where(lo, vv, zero), jnp.where(lo, zero, vv)], axis=0)
            v_aug = jnp.concatenate([v_bd, ones_bd], axis=1)
            q = jnp.concatenate(
                [q_ref[pl.ds(r0, BLOCK), (pair * Q_PER_KV + g) * BLOCK:(pair * Q_PER_KV + g + 1) * BLOCK]
                 for g in range(Q_PER_KV)], axis=0)
            s = lax.dot_general(q, k_bd, (((1,), (1,)), ((), ())),
                                preferred_element_type=F32)
            s = s + bias_ref[variant, pair]
            m_cols = []
            for g in range(Q_PER_KV):
                sg = s[g * BLOCK:(g + 1) * BLOCK]
                sink_a = sink_ref[(2 * pair) * Q_PER_KV + g]
                sink_b = sink_ref[(2 * pair + 1) * Q_PER_KV + g]
                m_a = jnp.maximum(jnp.max(sg[:, :2 * BLOCK], axis=-1, keepdims=True), sink_a)
                m_b = jnp.maximum(jnp.max(sg[:, 2 * BLOCK:], axis=-1, keepdims=True), sink_b)
                p_a = jnp.exp(sg[:, :2 * BLOCK] - m_a)
                p_b = jnp.exp(sg[:, 2 * BLOCK:] - m_b)
                p_scr[g * BLOCK:(g + 1) * BLOCK, :] = jnp.concatenate([p_a, p_b], axis=1).astype(BF16)
                m_cols.append(jnp.where(lo_o, jnp.exp(sink_a - m_a), jnp.exp(sink_b - m_b)))
            pv = jnp.dot(p_scr[...], v_aug, preferred_element_type=F32)
            for g in range(Q_PER_KV):
                num = pv[g * BLOCK:(g + 1) * BLOCK, :2 * HEAD_DIM]
                den = pv[g * BLOCK:(g + 1) * BLOCK, 2 * HEAD_DIM:] + m_cols[g]
                col = (pair * Q_PER_KV + g) * BLOCK
                o_ref[pl.ds(r0, BLOCK), col:col + BLOCK] = (num / den).astype(BF16)
        return carry

    lax.fori_loop(0, n_blocks, block, 0)


def _attention(qkv, sinks, bias, batch, seq, tq=512):
    t = qkv.shape[0]
    nt = seq // tq
    n_blocks = tq // BLOCK
    k_col = Q_WIDTH // KV_WIDTH
    v_col = k_col + 1

    def row(b, i):
        return b * nt + i

    def prev(b, i):
        return jnp.maximum(row(b, i) * n_blocks - 1, 0)

    return pl.pallas_call(
        functools.partial(_attn_kernel, n_blocks=n_blocks),
        out_shape=jax.ShapeDtypeStruct((t, Q_WIDTH), BF16),
        grid=(batch, nt),
        in_specs=[pl.BlockSpec(memory_space=pltpu.SMEM),
                  pl.BlockSpec((tq, Q_WIDTH), lambda b, i: (row(b, i), 0)),
                  pl.BlockSpec((tq, KV_WIDTH), lambda b, i: (row(b, i), k_col)),
                  pl.BlockSpec((tq, KV_WIDTH), lambda b, i: (row(b, i), v_col)),
                  pl.BlockSpec((BLOCK, KV_WIDTH), lambda b, i: (prev(b, i), k_col)),
                  pl.BlockSpec((BLOCK, KV_WIDTH), lambda b, i: (prev(b, i), v_col)),
                  _resident(bias.shape)],
        out_specs=pl.BlockSpec((tq, Q_WIDTH), lambda b, i: (row(b, i), 0)),
        scratch_shapes=[pltpu.VMEM((tq + BLOCK, KV_WIDTH), BF16),
                        pltpu.VMEM((tq + BLOCK, KV_WIDTH), BF16),
                        pltpu.VMEM((PAIR_ROWS, 4 * BLOCK), BF16)],
        compiler_params=_params(2),
        name="swa_attn",
    )(sinks, qkv, qkv, qkv, qkv, qkv, bias)


def _out_resid_kernel(a_ref, w_ref, x_ref, g_ref, o_ref):
    mix = jnp.dot(a_ref[...], w_ref[...], preferred_element_type=F32)
    o_ref[...] = x_ref[...] + _rms(mix, g_ref[...])


def _out_resid(a, w, x2, gain, tm=512):
    t, d = x2.shape
    k = a.shape[1]
    return pl.pallas_call(
        _out_resid_kernel,
        out_shape=jax.ShapeDtypeStruct((t, d), F32),
        grid=(t // tm,),
        in_specs=[pl.BlockSpec((tm, k), lambda i: (i, 0)),
                  _resident((k, d)),
                  pl.BlockSpec((tm, d), lambda i: (i, 0)),
                  _resident((1, d))],
        out_specs=pl.BlockSpec((tm, d), lambda i: (i, 0)),
        compiler_params=_params(1),
        name="out_resid",
    )(a, w, x2, gain)


def _ffn_kernel(x_ref, g_pre_ref, g_post_ref, wg_ref, wu_ref, wd_ref, o_ref, h_scr, *, n_chunk):
    k = pl.program_id(1)

    @pl.when(k == 0)
    def _():
        h_scr[...] = _rms(x_ref[...], g_pre_ref[...]).astype(BF16)
        o_ref[...] = jnp.zeros_like(o_ref)

    h = h_scr[...]
    gate = jnp.dot(h, wg_ref[...], preferred_element_type=F32)
    up = jnp.dot(h, wu_ref[...], preferred_element_type=F32)
    act = (gate * (1.0 / (1.0 + jnp.exp(-gate))) * up).astype(BF16)
    for n0 in range(0, o_ref.shape[1], n_chunk):
        o_ref[:, n0:n0 + n_chunk] += jnp.dot(act, wd_ref[:, n0:n0 + n_chunk], preferred_element_type=F32)

    @pl.when(k == pl.num_programs(1) - 1)
    def _():
        o_ref[...] = x_ref[...] + _rms(o_ref[...], g_post_ref[...])


def _ffn(x2, g_pre, g_post, w_gate_up, w_down, tm=512, th=512, n_chunk=512):
    t, d = x2.shape
    hid = w_down.shape[0]
    nk = hid // th
    return pl.pallas_call(
        functools.partial(_ffn_kernel, n_chunk=n_chunk),
        out_shape=jax.ShapeDtypeStruct((t, d), F32),
        grid=(t // tm, nk),
        in_specs=[pl.BlockSpec((tm, d), lambda i, k: (i, 0)),
                  _resident((1, d)),
                  _resident((1, d)),
                  pl.BlockSpec((d, th), lambda i, k: (0, k)),
                  pl.BlockSpec((d, th), lambda i, k: (0, k + nk)),
                  pl.BlockSpec((th, d), lambda i, k: (k, 0))],
        out_specs=pl.BlockSpec((tm, d), lambda i, k: (i, 0)),
        scratch_shapes=[pltpu.VMEM((tm, d), BF16)],
        compiler_params=_params(2),
        name="ffn",
    )(x2, g_pre, g_post, w_gate_up, w_gate_up, w_down)


def _gelu(x):
    c = math.sqrt(2.0 / math.pi)
    return 0.5 * x * (1.0 + jnp.tanh(c * (x + 0.044715 * (x * x * x))))


def _gmlp_in_kernel(x_ref, g_ref, w_ref, lng_ref, lnb_ref, u_ref, v_ref, h_scr, v_scr, *, n_chunk):
    inner = u_ref.shape[1]
    h_scr[...] = _rms(x_ref[...], g_ref[...]).astype(BF16)
    for n0 in range(0, inner, n_chunk):
        zu = jnp.dot(h_scr[...], w_ref[:, n0:n0 + n_chunk], preferred_element_type=F32)
        u_ref[:, n0:n0 + n_chunk] = _gelu(zu).astype(BF16)
        zv = jnp.dot(h_scr[...], w_ref[:, inner + n0:inner + n0 + n_chunk], preferred_element_type=F32)
        v_scr[:, n0:n0 + n_chunk] = _gelu(zv)
    v = v_scr[...]
    mu = jnp.mean(v, axis=-1, keepdims=True)
    vc = v - mu
    var = jnp.mean(vc * vc, axis=-1, keepdims=True)
    v_ref[...] = (vc * lax.rsqrt(var + LN_EPS) * lng_ref[...] + lnb_ref[...]).astype(BF16)


def _gmlp_in(x2, gain, w_in, ln_gain, ln_bias, tm=512, n_chunk=512):
    t, d = x2.shape
    inner = w_in.shape[1] // 2
    out = jax.ShapeDtypeStruct((t, inner), BF16)
    return pl.pallas_call(
        functools.partial(_gmlp_in_kernel, n_chunk=n_chunk),
        out_shape=(out, out),
        grid=(t // tm,),
        in_specs=[pl.BlockSpec((tm, d), lambda i: (i, 0)),
                  _resident((1, d)),
                  _resident((d, 2 * inner)),
                  _resident((1, inner)),
                  _resident((1, inner))],
        out_specs=(pl.BlockSpec((tm, inner), lambda i: (i, 0)),
                   pl.BlockSpec((tm, inner), lambda i: (i, 0))),
        scratch_shapes=[pltpu.VMEM((tm, d), BF16), pltpu.VMEM((tm, inner), F32)],
        compiler_params=_params(1),
        name="gmlp_in",
    )(x2, gain, w_in, ln_gain, ln_bias)


def _gmlp_gate_kernel(u_ref, v_ref, ws_ref, bs_ref, o_ref):
    tm = u_ref.shape[0]
    causal = (lax.broadcasted_iota(jnp.int32, (BLOCK, BLOCK), 0)
              >= lax.broadcasted_iota(jnp.int32, (BLOCK, BLOCK), 1))
    for g in range(GMLP_GROUPS):
        c0 = g * GROUP_DIM
        w = jnp.where(causal, ws_ref[g], jnp.zeros((BLOCK, BLOCK), BF16))
        b = bs_ref[g]
        for n in range(tm // BLOCK):
            r0 = n * BLOCK
            mixed = jnp.dot(w, v_ref[r0:r0 + BLOCK, c0:c0 + GROUP_DIM], preferred_element_type=F32) + b
            o_ref[r0:r0 + BLOCK, c0:c0 + GROUP_DIM] = (
                u_ref[r0:r0 + BLOCK, c0:c0 + GROUP_DIM].astype(F32) * mixed).astype(BF16)


def _gmlp_gate(u, v, ws, bs, tm=512):
    t, inner = u.shape
    return pl.pallas_call(
        _gmlp_gate_kernel,
        out_shape=jax.ShapeDtypeStruct((t, inner), BF16),
        grid=(t // tm,),
        in_specs=[pl.BlockSpec((tm, inner), lambda i: (i, 0)),
                  pl.BlockSpec((tm, inner), lambda i: (i, 0)),
                  _resident(ws.shape),
                  _resident(bs.shape)],
        out_specs=pl.BlockSpec((tm, inner), lambda i: (i, 0)),
        compiler_params=_params(1),
        name="gmlp_gate",
    )(u, v, ws, bs)


def _pair_major(w, axis):
    shape = w.shape
    lead, tail = shape[:axis], shape[axis + 1:]
    w = w.reshape(lead + (N_KV_PAIRS, 2, Q_PER_KV, HEAD_DIM) + tail)
    a = len(lead)
    perm = tuple(range(a)) + (a, a + 2, a + 1, a + 3) + tuple(range(a + 4, a + 4 + len(tail)))
    return w.transpose(perm).reshape(shape)


def kernel(x, attn_w_qkv, attn_w_o, attn_sinks, rel_bias_table, gmlp_w_in, gmlp_ln_gain, gmlp_ln_bias,
           gmlp_w_spatial, gmlp_b_spatial, gmlp_w_out, norm_gains, ffn_w_gate_up, ffn_w_down):
    batch, seq, d = x.shape
    x2 = x.reshape(batch * seq, d)
    gains = norm_gains.reshape(norm_gains.shape[0], 4, 1, d)

    w_qkv = attn_w_qkv[0]
    w_q = _pair_major(w_qkv[:, :Q_WIDTH] * (HEAD_DIM ** -0.5), 1)
    w_qkv_b = jnp.concatenate([w_q, w_qkv[:, Q_WIDTH:]], axis=1).astype(BF16)
    w_o_b = _pair_major(attn_w_o[0], 0).astype(BF16)
    bias = _rel_bias(rel_bias_table)
    qkv = _qkv_proj(x2, gains[0, 0], w_qkv_b)
    attn = _attention(qkv, attn_sinks[0], bias, batch, seq)
    x2 = _out_resid(attn, w_o_b, x2, gains[0, 1])
    x2 = _ffn(x2, gains[0, 2], gains[0, 3], ffn_w_gate_up[0].astype(BF16), ffn_w_down[0].astype(BF16))

    ws = gmlp_w_spatial[0].astype(BF16)
    bs = gmlp_b_spatial[0][:, :, None]
    u, v = _gmlp_in(x2, gains[1, 0], gmlp_w_in[0].astype(BF16),
                    gmlp_ln_gain[0][None, :], gmlp_ln_bias[0][None, :])
    gated = _gmlp_gate(u, v, ws, bs)
    x2 = _out_resid(gated, gmlp_w_out[0].astype(BF16), x2, gains[1, 1])
    x2 = _ffn(x2, gains[1, 2], gains[1, 3], ffn_w_gate_up[1].astype(BF16), ffn_w_down[1].astype(BF16))
    return x2.reshape(batch, seq, d)
```

```python
import functools
import math

import jax
import jax.numpy as jnp
from jax import lax
from jax.experimental import pallas as pl
from jax.experimental.pallas import tpu as pltpu

F32 = jnp.float32
BF16 = jnp.bfloat16

D_MODEL = 2048
HEAD_DIM = 64
N_Q_HEADS = 32
N_KV_HEADS = 4
Q_PER_KV = 8
BLOCK = 128
Q_WIDTH = N_Q_HEADS * HEAD_DIM
KV_WIDTH = N_KV_HEADS * HEAD_DIM
QKV_WIDTH = Q_WIDTH + 2 * KV_WIDTH
NUM_BUCKETS = 32
MAX_DISTANCE = 128
GMLP_GROUPS = 16
GROUP_DIM = 128
FFN_HIDDEN = 5632
NORM_EPS = 1e-6
LN_EPS = 1e-5
NEG_INF = -1e30
N_KV_PAIRS = N_KV_HEADS // 2
PAIR_ROWS = Q_PER_KV * BLOCK

VMEM_LIMIT = 56 * 1024 * 1024
FFN_VMEM_LIMIT = 62 * 1024 * 1024


def _params(n_axes, vmem=VMEM_LIMIT):
    return pltpu.CompilerParams(dimension_semantics=("arbitrary",) * n_axes,
                                vmem_limit_bytes=vmem)


def _resident(shape):
    nd = len(shape)
    return pl.BlockSpec(shape, lambda *_: (0,) * nd, pipeline_mode=pl.Buffered(1))


def _rms(x, gain):
    ms = jnp.mean(x * x, axis=-1, keepdims=True)
    return x * lax.rsqrt(ms + NORM_EPS) * gain


def _rel_bias_kernel(table_ref, bucket_ref, out_ref):
    pair = pl.program_id(0)
    bucket = bucket_ref[...]
    is_current = lax.broadcasted_iota(jnp.int32, bucket.shape, 1) >= BLOCK
    for parity in range(2):
        def head(g, carry):
            h = (2 * pair + parity) * Q_PER_KV + g
            acc = jnp.full(bucket.shape, NEG_INF, F32)
            for b in range(NUM_BUCKETS):
                acc = jnp.where(bucket == b, table_ref[b, h], acc)
            r0 = pl.multiple_of(g * BLOCK, BLOCK)
            cols = slice(parity * 2 * BLOCK, (parity + 1) * 2 * BLOCK)
            out_ref[0, 0, pl.ds(r0, BLOCK), cols] = acc
            out_ref[1, 0, pl.ds(r0, BLOCK), cols] = jnp.where(is_current, acc, NEG_INF)
            return carry
        lax.fori_loop(0, Q_PER_KV, head, 0)


def _rel_bias(rel_bias_table):
    qi = jnp.arange(BLOCK, dtype=jnp.int32)[:, None] + BLOCK
    kj = jnp.arange(2 * BLOCK, dtype=jnp.int32)[None, :]
    dist = qi - kj
    d = jnp.maximum(dist, 0)
    max_exact = NUM_BUCKETS // 2
    d_f = jnp.maximum(d, 1).astype(F32)
    large = max_exact + (jnp.log(d_f / max_exact) / math.log(MAX_DISTANCE / max_exact)
                         * (NUM_BUCKETS - max_exact)).astype(jnp.int32)
    large = jnp.minimum(large, NUM_BUCKETS - 1)
    bucket = jnp.where(d < max_exact, d, large)
    in_window = (dist >= 0) & (dist < BLOCK)
    buckets = jnp.where(in_window, bucket, -1).astype(jnp.int32)
    return pl.pallas_call(
        _rel_bias_kernel,
        out_shape=jax.ShapeDtypeStruct((2, N_KV_PAIRS, PAIR_ROWS, 4 * BLOCK), F32),
        grid=(N_KV_PAIRS,),
        in_specs=[pl.BlockSpec(memory_space=pltpu.SMEM),
                  pl.BlockSpec((BLOCK, 2 * BLOCK), lambda p: (0, 0))],
        out_specs=pl.BlockSpec((2, 1, PAIR_ROWS, 4 * BLOCK), lambda p: (0, p, 0, 0)),
        compiler_params=_params(1),
        name="rel_bias",
    )(rel_bias_table, buckets)


def _qkv_kernel(x_ref, g_ref, w_ref, o_ref, h_scr, *, n_chunk):
    h_scr[...] = _rms(x_ref[...], g_ref[...]).astype(BF16)
    for n0 in range(0, o_ref.shape[1], n_chunk):
        o_ref[:, n0:n0 + n_chunk] = jnp.dot(
            h_scr[...], w_ref[:, n0:n0 + n_chunk], preferred_element_type=F32).astype(BF16)


def _qkv_proj(x2, gain, w, tm=1024, n_chunk=512):
    t, d = x2.shape
    n = w.shape[1]
    return pl.pallas_call(
        functools.partial(_qkv_kernel, n_chunk=n_chunk),
        out_shape=jax.ShapeDtypeStruct((t, n), BF16),
        grid=(t // tm,),
        in_specs=[pl.BlockSpec((tm, d), lambda i: (i, 0)),
                  _resident((1, d)),
                  _resident((d, n))],
        out_specs=pl.BlockSpec((tm, n), lambda i: (i, 0)),
        scratch_shapes=[pltpu.VMEM((tm, d), BF16)],
        compiler_params=_params(1),
        name="qkv_proj",
    )(x2, gain, w)


def _attn_kernel(sink_ref, q_ref, k_ref, v_ref, kp_ref, vp_ref, bias_ref, o_ref, kbuf, vbuf, p_scr,
                 *, n_blocks):
    i = pl.program_id(1)
    tq = q_ref.shape[0]
    has_prev = i > 0
    kbuf[0:BLOCK] = jnp.where(has_prev, kp_ref[...], jnp.zeros_like(kp_ref))
    vbuf[0:BLOCK] = jnp.where(has_prev, vp_ref[...], jnp.zeros_like(vp_ref))
    kbuf[BLOCK:BLOCK + tq] = k_ref[...]
    vbuf[BLOCK:BLOCK + tq] = v_ref[...]

    lane = lax.broadcasted_iota(jnp.int32, (2 * BLOCK, 2 * HEAD_DIM), 1)
    lo = lane < HEAD_DIM
    row = lax.broadcasted_iota(jnp.int32, (4 * BLOCK, 2 * HEAD_DIM), 0)
    lane2 = lax.broadcasted_iota(jnp.int32, (4 * BLOCK, 2 * HEAD_DIM), 1)
    ones_bd = jnp.where((row < 2 * BLOCK) == (lane2 < HEAD_DIM), 1.0, 0.0).astype(BF16)
    lane_o = lax.broadcasted_iota(jnp.int32, (BLOCK, 2 * HEAD_DIM), 1)
    lo_o = lane_o < HEAD_DIM

    def block(j, carry):
        r0 = pl.multiple_of(j * BLOCK, BLOCK)
        variant = jnp.where(jnp.logical_and(i == 0, j == 0), 1, 0)
        for pair in range(N_KV_PAIRS):
            c0 = pair * 2 * HEAD_DIM
            kk = kbuf[pl.ds(r0, 2 * BLOCK), c0:c0 + 2 * HEAD_DIM]
            vv = vbuf[pl.ds(r0, 2 * BLOCK), c0:c0 + 2 * HEAD_DIM]
            zero = jnp.zeros_like(kk)
            k_bd = jnp.concatenate([jnp.where(lo, kk, zero), jnp.where(lo, zero, kk)], axis=0)
            v_bd = jnp.concatenate([jnp.where(lo, vv, zero), jnp.where(lo, zero, vv)], axis=0)
            v_aug = jnp.concatenate([v_bd, ones_bd], axis=1)
            q = jnp.concatenate(
                [q_ref[pl.ds(r0, BLOCK), (pair * Q_PER_KV + g) * BLOCK:(pair * Q_PER_KV + g + 1) * BLOCK]
                 for g in range(Q_PER_KV)], axis=0)
            s = lax.dot_general(q, k_bd, (((1,), (1,)), ((), ())),
                                preferred_element_type=F32)
            s = s + bias_ref[variant, pair]
            m_cols = []
            for g in range(Q_PER_KV):
                sg = s[g * BLOCK:(g + 1) * BLOCK]
                sink_a = sink_ref[(2 * pair) * Q_PER_KV + g]
                sink_b = sink_ref[(2 * pair + 1) * Q_PER_KV + g]
                m_a = jnp.maximum(jnp.max(sg[:, :2 * BLOCK], axis=-1, keepdims=True), sink_a)
                m_b = jnp.maximum(jnp.max(sg[:, 2 * BLOCK:], axis=-1, keepdims=True), sink_b)
                p_a = jnp.exp(sg[:, :2 * BLOCK] - m_a)
                p_b = jnp.exp(sg[:, 2 * BLOCK:] - m_b)
                p_scr[g * BLOCK:(g + 1) * BLOCK, :] = jnp.concatenate([p_a, p_b], axis=1).astype(BF16)
                m_cols.append(jnp.where(lo_o, jnp.exp(sink_a - m_a), jnp.exp(sink_b - m_b)))
            pv = jnp.dot(p_scr[...], v_aug, preferred_element_type=F32)
            for g in range(Q_PER_KV):
                num = pv[g * BLOCK:(g + 1) * BLOCK, :2 * HEAD_DIM]
                den = pv[g * BLOCK:(g + 1) * BLOCK, 2 * HEAD_DIM:] + m_cols[g]
                col = (pair * Q_PER_KV + g) * BLOCK
                o_ref[pl.ds(r0, BLOCK), col:col + BLOCK] = (num / den).astype(BF16)
        return carry

    lax.fori_loop(0, n_blocks, block, 0)


def _attention(qkv, sinks, bias, batch, seq, tq=512):
    t = qkv.shape[0]
    nt = seq // tq
    n_blocks = tq // BLOCK
    k_col = Q_WIDTH // KV_WIDTH
    v_col = k_col + 1

    def row(b, i):
        return b * nt + i

    def prev(b, i):
        return jnp.maximum(row(b, i) * n_blocks - 1, 0)

    return pl.pallas_call(
        functools.partial(_attn_kernel, n_blocks=n_blocks),
        out_shape=jax.ShapeDtypeStruct((t, Q_WIDTH), BF16),
        grid=(batch, nt),
        in_specs=[pl.BlockSpec(memory_space=pltpu.SMEM),
                  pl.BlockSpec((tq, Q_WIDTH), lambda b, i: (row(b, i), 0)),
                  pl.BlockSpec((tq, KV_WIDTH), lambda b, i: (row(b, i), k_col)),
                  pl.BlockSpec((tq, KV_WIDTH), lambda b, i: (row(b, i), v_col)),
                  pl.BlockSpec((BLOCK, KV_WIDTH), lambda b, i: (prev(b, i), k_col)),
                  pl.BlockSpec((BLOCK, KV_WIDTH), lambda b, i: (prev(b, i), v_col)),
                  _resident(bias.shape)],
        out_specs=pl.BlockSpec((tq, Q_WIDTH), lambda b, i: (row(b, i), 0)),
        scratch_shapes=[pltpu.VMEM((tq + BLOCK, KV_WIDTH), BF16),
                        pltpu.VMEM((tq + BLOCK, KV_WIDTH), BF16),
                        pltpu.VMEM((PAIR_ROWS, 4 * BLOCK), BF16)],
        compiler_params=_params(2),
        name="swa_attn",
    )(sinks, qkv, qkv, qkv, qkv, qkv, bias)


def _out_resid_kernel(a_ref, w_ref, x_ref, g_ref, o_ref):
    mix = jnp.dot(a_ref[...], w_ref[...], preferred_element_type=F32)
    o_ref[...] = x_ref[...] + _rms(mix, g_ref[...])


def _out_resid(a, w, x2, gain, tm=512):
    t, d = x2.shape
    k = a.shape[1]
    return pl.pallas_call(
        _out_resid_kernel,
        out_shape=jax.ShapeDtypeStruct((t, d), F32),
        grid=(t // tm,),
        in_specs=[pl.BlockSpec((tm, k), lambda i: (i, 0)),
                  _resident((k, d)),
                  pl.BlockSpec((tm, d), lambda i: (i, 0)),
                  _resident((1, d))],
        out_specs=pl.BlockSpec((tm, d), lambda i: (i, 0)),
        compiler_params=_params(1),
        name="out_resid",
    )(a, w, x2, gain)


def _ffn_kernel(x_ref, g_pre_ref, g_post_ref, wg_ref, wu_ref, wd_ref, o_ref, h_scr, act_scr,
                *, h_chunk, n_chunk, r_chunk):
    k = pl.program_id(1)
    tm, d = o_ref.shape
    th = act_scr.shape[1]

    @pl.when(k == 0)
    def _():
        for r0 in range(0, tm, r_chunk):
            h_scr[r0:r0 + r_chunk] = _rms(x_ref[r0:r0 + r_chunk], g_pre_ref[...]).astype(BF16)
        o_ref[...] = jnp.zeros_like(o_ref)

    for c0 in range(0, th, h_chunk):
        gate = jnp.dot(h_scr[...], wg_ref[:, c0:c0 + h_chunk], preferred_element_type=F32)
        up = jnp.dot(h_scr[...], wu_ref[:, c0:c0 + h_chunk], preferred_element_type=F32)
        act_scr[:, c0:c0 + h_chunk] = (gate * (1.0 / (1.0 + jnp.exp(-gate))) * up).astype(BF16)
    for n0 in range(0, d, n_chunk):
        o_ref[:, n0:n0 + n_chunk] += jnp.dot(act_scr[...], wd_ref[:, n0:n0 + n_chunk],
                                             preferred_element_type=F32)

    @pl.when(k == pl.num_programs(1) - 1)
    def _():
        for r0 in range(0, tm, r_chunk):
            o_ref[r0:r0 + r_chunk] = x_ref[r0:r0 + r_chunk] + _rms(o_ref[r0:r0 + r_chunk], g_post_ref[...])


def _ffn(x2, g_pre, g_post, w_gate_up, w_down, tm=1024, th=512, h_chunk=256, n_chunk=256, r_chunk=128):
    t, d = x2.shape
    hid = w_down.shape[0]
    nk = hid // th
    return pl.pallas_call(
        functools.partial(_ffn_kernel, h_chunk=h_chunk, n_chunk=n_chunk, r_chunk=r_chunk),
        out_shape=jax.ShapeDtypeStruct((t, d), F32),
        grid=(t // tm, nk),
        in_specs=[pl.BlockSpec((tm, d), lambda i, k: (i, 0)),
                  _resident((1, d)),
                  _resident((1, d)),
                  pl.BlockSpec((d, th), lambda i, k: (0, k)),
                  pl.BlockSpec((d, th), lambda i, k: (0, k + nk)),
                  pl.BlockSpec((th, d), lambda i, k: (k, 0))],
        out_specs=pl.BlockSpec((tm, d), lambda i, k: (i, 0)),
        scratch_shapes=[pltpu.VMEM((tm, d), BF16), pltpu.VMEM((tm, th), BF16)],
        compiler_params=_params(2, FFN_VMEM_LIMIT),
        name="ffn",
    )(x2, g_pre, g_post, w_gate_up, w_gate_up, w_down)


def _gelu(x):
    c = math.sqrt(2.0 / math.pi)
    return 0.5 * x * (1.0 + jnp.tanh(c * (x + 0.044715 * (x * x * x))))


def _gmlp_in_kernel(x_ref, g_ref, w_ref, lng_ref, lnb_ref, u_ref, v_ref, h_scr, v_scr, *, n_chunk):
    inner = u_ref.shape[1]
    h_scr[...] = _rms(x_ref[...], g_ref[...]).astype(BF16)
    for n0 in range(0, inner, n_chunk):
        zu = jnp.dot(h_scr[...], w_ref[:, n0:n0 + n_chunk], preferred_element_type=F32)
        u_ref[:, n0:n0 + n_chunk] = _gelu(zu).astype(BF16)
        zv = jnp.dot(h_scr[...], w_ref[:, inner + n0:inner + n0 + n_chunk], preferred_element_type=F32)
        v_scr[:, n0:n0 + n_chunk] = _gelu(zv)
    v = v_scr[...]
    mu = jnp.mean(v, axis=-1, keepdims=True)
    vc = v - mu
    var = jnp.mean(vc * vc, axis=-1, keepdims=True)
    v_ref[...] = (vc * lax.rsqrt(var + LN_EPS) * lng_ref[...] + lnb_ref[...]).astype(BF16)


def _gmlp_in(x2, gain, w_in, ln_gain, ln_bias, tm=512, n_chunk=512):
    t, d = x2.shape
    inner = w_in.shape[1] // 2
    out = jax.ShapeDtypeStruct((t, inner), BF16)
    return pl.pallas_call(
        functools.partial(_gmlp_in_kernel, n_chunk=n_chunk),
        out_shape=(out, out),
        grid=(t // tm,),
        in_specs=[pl.BlockSpec((tm, d), lambda i: (i, 0)),
                  _resident((1, d)),
                  _resident((d, 2 * inner)),
                  _resident((1, inner)),
                  _resident((1, inner))],
        out_specs=(pl.BlockSpec((tm, inner), lambda i: (i, 0)),
                   pl.BlockSpec((tm, inner), lambda i: (i, 0))),
        scratch_shapes=[pltpu.VMEM((tm, d), BF16), pltpu.VMEM((tm, inner), F32)],
        compiler_params=_params(1),
        name="gmlp_in",
    )(x2, gain, w_in, ln_gain, ln_bias)


def _gmlp_out_kernel(u_ref, v_ref, ws_ref, bs_ref, w_ref, x_ref, g_ref, o_ref, gated_scr, *, k_chunk):
    tm, inner = u_ref.shape
    causal = (lax.broadcasted_iota(jnp.int32, (BLOCK, BLOCK), 0)
              >= lax.broadcasted_iota(jnp.int32, (BLOCK, BLOCK), 1))
    mix = None
    for k0 in range(0, inner, k_chunk):
        for g in range(k0 // GROUP_DIM, (k0 + k_chunk) // GROUP_DIM):
            c0 = g * GROUP_DIM
            w = jnp.where(causal, ws_ref[g], jnp.zeros((BLOCK, BLOCK), BF16))
            b = bs_ref[g]
            for n in range(tm // BLOCK):
                r0 = n * BLOCK
                mixed = jnp.dot(w, v_ref[r0:r0 + BLOCK, c0:c0 + GROUP_DIM], preferred_element_type=F32) + b
                gated_scr[r0:r0 + BLOCK, c0:c0 + GROUP_DIM] = (
                    u_ref[r0:r0 + BLOCK, c0:c0 + GROUP_DIM].astype(F32) * mixed).astype(BF16)
        part = jnp.dot(gated_scr[:, k0:k0 + k_chunk], w_ref[k0:k0 + k_chunk, :], preferred_element_type=F32)
        mix = part if mix is None else mix + part
    o_ref[...] = x_ref[...] + _rms(mix, g_ref[...])


def _gmlp_out(u, v, ws, bs, w_out, x2, gain, tm=512, k_chunk=512):
    t, inner = u.shape
    d = x2.shape[1]
    return pl.pallas_call(
        functools.partial(_gmlp_out_kernel, k_chunk=k_chunk),
        out_shape=jax.ShapeDtypeStruct((t, d), F32),
        grid=(t // tm,),
        in_specs=[pl.BlockSpec((tm, inner), lambda i: (i, 0)),
                  pl.BlockSpec((tm, inner), lambda i: (i, 0)),
                  _resident(ws.shape),
                  _resident(bs.shape),
                  _resident((inner, d)),
                  pl.BlockSpec((tm, d), lambda i: (i, 0)),
                  _resident((1, d))],
        out_specs=pl.BlockSpec((tm, d), lambda i: (i, 0)),
        scratch_shapes=[pltpu.VMEM((tm, inner), BF16)],
        compiler_params=_params(1),
        name="gmlp_out",
    )(u, v, ws, bs, w_out, x2, gain)


def _pair_major(w, axis):
    shape = w.shape
    lead, tail = shape[:axis], shape[axis + 1:]
    w = w.reshape(lead + (N_KV_PAIRS, 2, Q_PER_KV, HEAD_DIM) + tail)
    a = len(lead)
    perm = tuple(range(a)) + (a, a + 2, a + 1, a + 3) + tuple(range(a + 4, a + 4 + len(tail)))
    return w.transpose(perm).reshape(shape)


def kernel(x, attn_w_qkv, attn_w_o, attn_sinks, rel_bias_table, gmlp_w_in, gmlp_ln_gain, gmlp_ln_bias,
           gmlp_w_spatial, gmlp_b_spatial, gmlp_w_out, norm_gains, ffn_w_gate_up, ffn_w_down):
    batch, seq, d = x.shape
    x2 = x.reshape(batch * seq, d)
    gains = norm_gains.reshape(norm_gains.shape[0], 4, 1, d)

    w_qkv = attn_w_qkv[0]
    w_q = _pair_major(w_qkv[:, :Q_WIDTH] * (HEAD_DIM ** -0.5), 1)
    w_qkv_b = jnp.concatenate([w_q, w_qkv[:, Q_WIDTH:]], axis=1).astype(BF16)
    w_o_b = _pair_major(attn_w_o[0], 0).astype(BF16)
    bias = _rel_bias(rel_bias_table)
    qkv = _qkv_proj(x2, gains[0, 0], w_qkv_b)
    attn = _attention(qkv, attn_sinks[0], bias, batch, seq)
    x2 = _out_resid(attn, w_o_b, x2, gains[0, 1])
    x2 = _ffn(x2, gains[0, 2], gains[0, 3], ffn_w_gate_up[0].astype(BF16), ffn_w_down[0].astype(BF16))

    ws = gmlp_w_spatial[0].astype(BF16)
    bs = gmlp_b_spatial[0][:, :, None]
    u, v = _gmlp_in(x2, gains[1, 0], gmlp_w_in[0].astype(BF16),
                    gmlp_ln_gain[0][None, :], gmlp_ln_bias[0][None, :])
    x2 = _gmlp_out(u, v, ws, bs, gmlp_w_out[0].astype(BF16), x2, gains[1, 1])
    x2 = _ffn(x2, gains[1, 2], gains[1, 3], ffn_w_gate_up[1].astype(BF16), ffn_w_down[1].astype(BF16))
    return x2.reshape(batch, seq, d)
```
